```python
import math, functools
import jax, jax.numpy as jnp
from jax import lax
import numpy as np

D_MODEL = 1024
BATCH = 32
SEQ = 256
DEPTH = 4
DEC_BATCH = 2
DEC_SEQ = 4096
PAST_LEN = 512

GRID_W = 64
N_MIXERS = 3
N_A = (DEPTH + 2) // 3
N_B = (DEPTH + 1) // 3
N_C = DEPTH // 3
Q_BLOCK = 128
ROPE_THETA = 10000.0
EPS = 1e-6
N_MOD = 9
D_FF = 2816
H_A = 16
Q_LORA = 384
KV_LORA = 256
NOPE_A = 64
ROPE_A = 32
V_A = 64
H_B = 16
KV_B = 4
HD_B = 64
H_C = 8
DK_C = 64
DV_C = 128

kernel_name = "hybrid_diffusion_mla_gqa_diff_macaron"


def _rms(x, g):
    xf = x.astype(jnp.float32)
    y = xf * lax.rsqrt(jnp.mean(xf * xf, axis=-1, keepdims=True) + EPS)
    return (y * g.astype(jnp.float32)).astype(x.dtype)


def _axial_rope_tables(n_tokens, rot_dim, dtype):
    rows = n_tokens // GRID_W
    r, cidx = jnp.meshgrid(jnp.arange(rows, dtype=jnp.float32),
                           jnp.arange(GRID_W, dtype=jnp.float32), indexing="ij")
    r, cidx = r.reshape(-1), cidx.reshape(-1)
    n_f = rot_dim // 4
    freqs = ROPE_THETA ** (-jnp.arange(n_f, dtype=jnp.float32) / n_f)
    ang = jnp.concatenate([r[:, None] * freqs, cidx[:, None] * freqs], axis=-1)
    return jnp.cos(ang).astype(dtype), jnp.sin(ang).astype(dtype)


def _rope(x, cos, sin):
    shp = (cos.shape[0],) + (1,) * (x.ndim - 3) + (cos.shape[1],)
    c, s = cos.reshape(shp), sin.reshape(shp)
    x1, x2 = jnp.split(x, 2, axis=-1)
    return jnp.concatenate([x1 * c - x2 * s, x1 * s + x2 * c], axis=-1)


def _gqa_sdpa(q, k, v):
    scale = q.shape[-1] ** -0.5
    s = jnp.einsum("bqhgd,bkhd->bhgqk", q, k).astype(jnp.float32) * scale
    p = jax.nn.softmax(s, axis=-1).astype(v.dtype)
    return jnp.einsum("bhgqk,bkhd->bqhgd", p, v)


def _diff_sdpa(q, k, v, lam):
    scale = q.shape[-1] ** -0.5
    s = jnp.einsum("bqhmd,bkhmd->bhmqk", q, k).astype(jnp.float32) * scale
    p = jax.nn.softmax(s, axis=-1)
    a = (p[:, :, 0] - lam * p[:, :, 1]).astype(v.dtype)
    return jnp.einsum("bhqk,bkhd->bqhd", a, v)


def _sweep(attn, q, k, v):
    b, sq = q.shape[0], q.shape[1]
    nb = sq // Q_BLOCK
    qb = jnp.moveaxis(q.reshape((b, nb, Q_BLOCK) + q.shape[2:]), 1, 0)
    o = lax.map(lambda blk: attn(blk, k, v), qb)
    return jnp.moveaxis(o, 0, 1).reshape((b, sq) + o.shape[3:])


def _ada(act, w, b):
    return jnp.split(act @ w + b, N_MOD, axis=-1)


def _modulate(x, g, shift, scale):
    return _rms(x, g) * (1 + scale[:, None]) + shift[:, None]


def _swiglu(h, w_in, w_out):
    gt, up = jnp.split(h @ w_in, 2, axis=-1)
    return (jax.nn.silu(gt) * up) @ w_out


def _ffn_sub(y, g, shift, scale, gate, w_in, w_out):
    return y + 0.5 * gate[:, None] * _swiglu(_modulate(y, g, shift, scale), w_in, w_out)


def _mla_down(h, w_down, g_q, g_kv):
    cq, ckv, kr = jnp.split(h @ w_down, [Q_LORA, Q_LORA + KV_LORA], axis=-1)
    return _rms(cq, g_q), _rms(ckv, g_kv), kr


def _mla_attend(cq, ckv, kr, w_uq, w_uk, w_uv, w_o, rope_q):
    b, sq = cq.shape[:2]
    sk = ckv.shape[1]
    q = (cq @ w_uq).reshape(b, sq, H_A, NOPE_A + ROPE_A)
    if rope_q is not None:
        q = jnp.concatenate([q[..., :NOPE_A], _rope(q[..., NOPE_A:], *rope_q)], axis=-1)
    k_nope = (ckv @ w_uk).reshape(b, sk, H_A, NOPE_A)
    k = jnp.concatenate([k_nope, jnp.broadcast_to(kr[:, :, None, :], (b, sk, H_A, ROPE_A))], axis=-1)
    v = (ckv @ w_uv).reshape(b, sk, H_A, V_A)
    o = _sweep(_gqa_sdpa, q[:, :, :, None, :], k, v)
    return o.reshape(b, sq, H_A * V_A) @ w_o


def _mla_context(h, w_down, g_q, g_kv, w_uq, w_uk, w_uv, w_o):
    cq, ckv, kr = _mla_down(h, w_down, g_q, g_kv)
    return _mla_attend(cq, ckv, kr, w_uq, w_uk, w_uv, w_o, None), ckv, kr


def _mla_latent(h, ckv_ctx, kr_ctx, w_down, g_q, g_kv, w_uq, w_uk, w_uv, w_o, rope):
    cq, ckv, kr = _mla_down(h, w_down, g_q, g_kv)
    kr = _rope(kr, *rope)
    ckv_all = jnp.concatenate([ckv, ckv_ctx], axis=1)
    kr_all = jnp.concatenate([kr, kr_ctx], axis=1)
    return _mla_attend(cq, ckv_all, kr_all, w_uq, w_uk, w_uv, w_o, rope)


def _gqa_qkv(h, w_qkv, g_q, g_k):
    b, s = h.shape[:2]
    q, k, v = jnp.split(h @ w_qkv, [H_B * HD_B, (H_B + KV_B) * HD_B], axis=-1)
    q = _rms(q.reshape(b, s, H_B, HD_B), g_q)
    k = _rms(k.reshape(b, s, KV_B, HD_B), g_k)
    return q, k, v.reshape(b, s, KV_B, HD_B)


def _gqa_out(q, k, v, w_o):
    b, s = q.shape[:2]
    o = _sweep(_gqa_sdpa, q.reshape(b, s, KV_B, H_B // KV_B, HD_B), k, v)
    return o.reshape(b, s, H_B * HD_B) @ w_o


def _diff_qkv(h, w_qkv):
    b, s = h.shape[:2]
    q, k, v = jnp.split(h @ w_qkv, [H_C * 2 * DK_C, 2 * H_C * 2 * DK_C], axis=-1)
    return (q.reshape(b, s, H_C, 2, DK_C), k.reshape(b, s, H_C, 2, DK_C),
            v.reshape(b, s, H_C, DV_C))


def _diff_out(q, k, v, lam, lam_init, g_sub, w_o):
    b, s = q.shape[:2]
    o = _sweep(functools.partial(_diff_sdpa, lam=lam), q, k, v)
    o = _rms(o, g_sub) * (1 - lam_init)
    return o.reshape(b, s, H_C * DV_C) @ w_o


def setup_inputs(seed: int = 0) -> dict:
    key = jax.random.key(seed)
    ks = iter(jax.random.split(key, 48))
    f32 = jnp.float32

    def nrm(shape, scale=1.0):
        return jax.random.normal(next(ks), shape, f32) * scale

    def gain(shape):
        return 1.0 + nrm(shape, 0.05)

    D = D_MODEL
    return {
        "x_prompt": nrm((BATCH, SEQ, D)),
        "x_sample": nrm((DEC_BATCH, DEC_SEQ, D)),
        "cache_mla_ckv": nrm((DEC_BATCH, N_A, PAST_LEN, KV_LORA)),
        "cache_mla_kr": nrm((DEC_BATCH, N_A, PAST_LEN, ROPE_A)),
        "cache_gqa_k": nrm((DEC_BATCH, N_B, PAST_LEN, KV_B, HD_B)),
        "cache_gqa_v": nrm((DEC_BATCH, N_B, PAST_LEN, KV_B, HD_B)),
        "cache_diff_k": nrm((DEC_BATCH, N_C, PAST_LEN, H_C, 2, DK_C)),
        "cache_diff_v": nrm((DEC_BATCH, N_C, PAST_LEN, H_C, DV_C)),
        "c": nrm((DEC_BATCH, D)),
        "c_ctx": nrm((D,)),
        "ada_w": nrm((DEPTH, D, N_MOD * D), 0.5 * D ** -0.5),
        "ada_b": nrm((DEPTH, N_MOD * D), 0.02),
        "norm_ffn1": gain((DEPTH, D)),
        "norm_mix": gain((DEPTH, D)),
        "norm_ffn2": gain((DEPTH, D)),
        "ffn1_w_in": nrm((DEPTH, D, 2 * D_FF), D ** -0.5),
        "ffn1_w_out": nrm((DEPTH, D_FF, D), D_FF ** -0.5),
        "ffn2_w_in": nrm((DEPTH, D, 2 * D_FF), D ** -0.5),
        "ffn2_w_out": nrm((DEPTH, D_FF, D), D_FF ** -0.5),
        "mla_w_down": nrm((N_A, D, Q_LORA + KV_LORA + ROPE_A), D ** -0.5),
        "mla_g_q": gain((N_A, Q_LORA)),
        "mla_g_kv": gain((N_A, KV_LORA)),
        "mla_w_uq": nrm((N_A, Q_LORA, H_A * (NOPE_A + ROPE_A)), Q_LORA ** -0.5),
        "mla_w_uk": nrm((N_A, KV_LORA, H_A * NOPE_A), KV_LORA ** -0.5),
        "mla_w_uv": nrm((N_A, KV_LORA, H_A * V_A), KV_LORA ** -0.5),
        "mla_w_o": nrm((N_A, H_A * V_A, D), (H_A * V_A) ** -0.5),
        "gqa_w_qkv": nrm((N_B, D, (H_B + 2 * KV_B) * HD_B), D ** -0.5),
        "gqa_g_q": gain((N_B, HD_B)),
        "gqa_g_k": gain((N_B, HD_B)),
        "gqa_w_o": nrm((N_B, H_B * HD_B, D), (H_B * HD_B) ** -0.5),
        "diff_w_qkv": nrm((N_C, D, 2 * H_C * 2 * DK_C + H_C * DV_C), D ** -0.5),
        "diff_lq1": nrm((N_C, DK_C), 0.1),
        "diff_lk1": nrm((N_C, DK_C), 0.1),
        "diff_lq2": nrm((N_C, DK_C), 0.1),
        "diff_lk2": nrm((N_C, DK_C), 0.1),
        "diff_g_sub": gain((N_C, DV_C)),
        "diff_w_o": nrm((N_C, H_C * DV_C, D), (H_C * DV_C) ** -0.5),
        "norm_final": gain((D,)),
    }


def reference(x_prompt, x_sample, cache_mla_ckv, cache_mla_kr, cache_gqa_k, cache_gqa_v,
              cache_diff_k, cache_diff_v, c, c_ctx,
              ada_w, ada_b, norm_ffn1, norm_mix, norm_ffn2,
              ffn1_w_in, ffn1_w_out, ffn2_w_in, ffn2_w_out,
              mla_w_down, mla_g_q, mla_g_kv, mla_w_uq, mla_w_uk, mla_w_uv, mla_w_o,
              gqa_w_qkv, gqa_g_q, gqa_g_k, gqa_w_o,
              diff_w_qkv, diff_lq1, diff_lk1, diff_lq2, diff_lk2, diff_g_sub, diff_w_o,
              norm_final):
    n_lat = x_sample.shape[1]
    rope_a = _axial_rope_tables(n_lat, ROPE_A, x_sample.dtype)
    rope_b = _axial_rope_tables(n_lat, HD_B, x_sample.dtype)
    rope_c = _axial_rope_tables(n_lat, DK_C, x_sample.dtype)

    y_ctx, y_lat = x_prompt, x_sample
    act_ctx = jax.nn.silu(c_ctx)[None, :]
    act_lat = jax.nn.silu(c)
    new_ckv, new_kr, new_gk, new_gv, new_dk, new_dv = [], [], [], [], [], []

    for i in range(DEPTH):
        kind, j = i % N_MIXERS, i // N_MIXERS
        mc = _ada(act_ctx, ada_w[i], ada_b[i])
        ml = _ada(act_lat, ada_w[i], ada_b[i])
        y_ctx = _ffn_sub(y_ctx, norm_ffn1[i], mc[0], mc[1], mc[2], ffn1_w_in[i], ffn1_w_out[i])
        y_lat = _ffn_sub(y_lat, norm_ffn1[i], ml[0], ml[1], ml[2], ffn1_w_in[i], ffn1_w_out[i])
        h_ctx = _modulate(y_ctx, norm_mix[i], mc[3], mc[4])
        h_lat = _modulate(y_lat, norm_mix[i], ml[3], ml[4])
        if kind == 0:
            pa = (mla_w_down[j], mla_g_q[j], mla_g_kv[j], mla_w_uq[j], mla_w_uk[j], mla_w_uv[j], mla_w_o[j])
            o_ctx, ckv, kr = _mla_context(h_ctx, *pa)
            o_lat = _mla_latent(h_lat, cache_mla_ckv[:, j], cache_mla_kr[:, j], *pa, rope_a)
            new_ckv.append(ckv)
            new_kr.append(kr)
        elif kind == 1:
            qc, kc, vc = _gqa_qkv(h_ctx, gqa_w_qkv[j], gqa_g_q[j], gqa_g_k[j])
            o_ctx = _gqa_out(qc, kc, vc, gqa_w_o[j])
            new_gk.append(kc)
            new_gv.append(vc)
            ql, kl, vl = _gqa_qkv(h_lat, gqa_w_qkv[j], gqa_g_q[j], gqa_g_k[j])
            ql, kl = _rope(ql, *rope_b), _rope(kl, *rope_b)
            k_all = jnp.concatenate([kl, cache_gqa_k[:, j]], axis=1)
            v_all = jnp.concatenate([vl, cache_gqa_v[:, j]], axis=1)
            o_lat = _gqa_out(ql, k_all, v_all, gqa_w_o[j])
        else:
            lam_init = 0.8 - 0.6 * math.exp(-0.3 * i)
            lam = (jnp.exp(jnp.sum(diff_lq1[j].astype(jnp.float32) * diff_lk1[j].astype(jnp.float32)))
                   - jnp.exp(jnp.sum(diff_lq2[j].astype(jnp.float32) * diff_lk2[j].astype(jnp.float32)))
                   + lam_init)
            qc, kc, vc = _diff_qkv(h_ctx, diff_w_qkv[j])
            o_ctx = _diff_out(qc, kc, vc, lam, lam_init, diff_g_sub[j], diff_w_o[j])
            new_dk.append(kc)
            new_dv.append(vc)
            ql, kl, vl = _diff_qkv(h_lat, diff_w_qkv[j])
            ql, kl = _rope(ql, *rope_c), _rope(kl, *rope_c)
            k_all = jnp.concatenate([kl, cache_diff_k[:, j]], axis=1)
            v_all = jnp.concatenate([vl, cache_diff_v[:, j]], axis=1)
            o_lat = _diff_out(ql, k_all, v_all, lam, lam_init, diff_g_sub[j], diff_w_o[j])
        y_ctx = y_ctx + mc[5][:, None] * o_ctx
        y_lat = y_lat + ml[5][:, None] * o_lat
        y_ctx = _ffn_sub(y_ctx, norm_ffn2[i], mc[6], mc[7], mc[8], ffn2_w_in[i], ffn2_w_out[i])
        y_lat = _ffn_sub(y_lat, norm_ffn2[i], ml[6], ml[7], ml[8], ffn2_w_in[i], ffn2_w_out[i])

    y_prompt = _rms(y_ctx, norm_final)
    y_sample = _rms(y_lat, norm_final)
    new_mla_ckv = jnp.stack(new_ckv, axis=1)
    new_mla_kr = jnp.stack(new_kr, axis=1)
    new_gqa_k = jnp.stack(new_gk, axis=1)
    new_gqa_v = jnp.stack(new_gv, axis=1)
    new_diff_k = jnp.stack(new_dk, axis=1)
    new_diff_v = jnp.stack(new_dv, axis=1)
    return (y_prompt, y_sample, new_mla_ckv, new_mla_kr, new_gqa_k, new_gqa_v, new_diff_k, new_diff_v)
```

```python
import functools
import math

import jax
import jax.numpy as jnp
from jax import lax
from jax.experimental import pallas as pl
from jax.experimental.pallas import tpu as pltpu

D_MODEL = 1024
BATCH = 32
SEQ = 256
DEPTH = 4
DEC_BATCH = 2
DEC_SEQ = 4096
PAST_LEN = 512
GRID_W = 64
N_MIXERS = 3
ROPE_THETA = 10000.0
EPS = 1e-6
N_MOD = 9
D_FF = 2816
H_A = 16
Q_LORA = 384
KV_LORA = 256
NOPE_A = 64
ROPE_A = 32
V_A = 64
H_B = 16
KV_B = 4
HD_B = 64
H_C = 8
DK_C = 64
DV_C = 128

N_CTX = BATCH * SEQ
N_LAT = DEC_BATCH * DEC_SEQ
N_TOK = N_CTX + N_LAT
LANE = 128
MOD_ROWS = 8
VMEM_LIMIT = 56 * 1024 * 1024
F32 = jnp.float32
BF16 = jnp.bfloat16

TM = 512
TQ = 256
FF_CHUNK = 1408


def _params(*sem):
    return pltpu.CompilerParams(dimension_semantics=sem, vmem_limit_bytes=VMEM_LIMIT)


def _mod_spec(layer, k, tm, t0):
    tiles_ctx = N_CTX // tm
    tiles_per_lat = DEC_SEQ // tm

    def imap(t, *_):
        tg = t + t0
        row = jnp.where(tg < tiles_ctx, 0, 1 + (tg - tiles_ctx) // tiles_per_lat)
        return (layer, k, row, 0, 0)

    return pl.BlockSpec((None, None, None, 1, D_MODEL), imap)


def _row_spec(n):
    return pl.BlockSpec((1, n), lambda *_: (0, 0))


def _resident(shape):
    return pl.BlockSpec(shape, lambda *_: (0,) * len(shape))


def _layer_resident(shape, layer):
    return pl.BlockSpec((None,) + shape, lambda *_: (layer,) + (0,) * len(shape))


def _rms_mod(x, g, shift, scale):
    ms = jnp.mean(x * x, axis=-1, keepdims=True)
    return (x * lax.rsqrt(ms + EPS) * g) * (1.0 + scale) + shift


def _rms(x, g):
    ms = jnp.mean(x * x, axis=-1, keepdims=True)
    return x * lax.rsqrt(ms + EPS) * g


def _dot(a, b):
    return jnp.dot(a, b, preferred_element_type=F32)


def _dot_nt(a, b):
    return lax.dot_general(a, b, (((1,), (1,)), ((), ())), preferred_element_type=F32)


def _ada_kernel(c_ref, w_ref, b_ref, o_ref):
    c = c_ref[...]
    act = c * jax.nn.sigmoid(c)
    o_ref[...] = jnp.dot(act, w_ref[...], precision=lax.Precision.HIGHEST,
                         preferred_element_type=F32) + b_ref[...]


def _ada(c_rows, ada_w, ada_b):
    out = pl.pallas_call(
        _ada_kernel,
        grid=(DEPTH, N_MOD),
        in_specs=[
            _resident((MOD_ROWS, D_MODEL)),
            pl.BlockSpec((None, D_MODEL, D_MODEL), lambda i, k: (i, 0, k)),
            pl.BlockSpec((None, None, 1, D_MODEL), lambda i, k: (i, k, 0, 0)),
        ],
        out_specs=pl.BlockSpec((None, None, MOD_ROWS, D_MODEL), lambda i, k: (i, k, 0, 0)),
        out_shape=jax.ShapeDtypeStruct((DEPTH, N_MOD, MOD_ROWS, D_MODEL), F32),
        compiler_params=_params("parallel", "parallel"),
        name="ada_mod",
    )(c_rows, ada_w, ada_b.reshape(DEPTH, N_MOD, 1, D_MODEL))
    return out.reshape(DEPTH, N_MOD, MOD_ROWS, 1, D_MODEL)


def _ffn_kernel(y_ref, g_ref, sh_ref, sc_ref, gate_ref, win_ref, wout_ref, o_ref, a_ref):
    x = y_ref[...]
    h = _rms_mod(x, g_ref[...], sh_ref[...], sc_ref[...]).astype(BF16)
    for c in range(D_FF // FF_CHUNK):
        lo = c * FF_CHUNK
        gt = _dot(h, win_ref[:, lo:lo + FF_CHUNK])
        up = _dot(h, win_ref[:, D_FF + lo:D_FF + lo + FF_CHUNK])
        a_ref[:, lo:lo + FF_CHUNK] = (gt * jax.nn.sigmoid(gt) * up).astype(BF16)
    f = _dot(a_ref[...], wout_ref[...])
    o_ref[...] = x + (0.5 * gate_ref[...]) * f


def _ffn(y, norm_g, mod, layer, k0, w_in, w_out):
    tok = pl.BlockSpec((TM, D_MODEL), lambda t: (t, 0))
    return pl.pallas_call(
        _ffn_kernel,
        grid=(N_TOK // TM,),
        in_specs=[
            tok,
            pl.BlockSpec((None, 1, D_MODEL), lambda t: (layer, 0, 0)),
            _mod_spec(layer, k0, TM, 0),
            _mod_spec(layer, k0 + 1, TM, 0),
            _mod_spec(layer, k0 + 2, TM, 0),
            pl.BlockSpec((None, D_MODEL, 2 * D_FF), lambda t: (layer, 0, 0),
                         pipeline_mode=pl.Buffered(1)),
            pl.BlockSpec((None, D_FF, D_MODEL), lambda t: (layer, 0, 0),
                         pipeline_mode=pl.Buffered(1)),
        ],
        out_specs=tok,
        out_shape=jax.ShapeDtypeStruct((N_TOK, D_MODEL), F32),
        scratch_shapes=[pltpu.VMEM((TM, D_FF), BF16)],
        compiler_params=_params("parallel"),
        name="ffn",
    )(y, norm_g.reshape(DEPTH, 1, D_MODEL), mod, mod, mod, w_in, w_out)


def _rope_block(x, c, sa, sb, half):
    return x * c + pltpu.roll(x, half, 1) * sa + pltpu.roll(x, LANE - half, 1) * sb


def _axial_angles(rot_dim):
    pos = jnp.arange(DEC_SEQ, dtype=jnp.int32)
    r = (pos // GRID_W).astype(F32)
    cidx = (pos % GRID_W).astype(F32)
    n_f = rot_dim // 4
    freqs = ROPE_THETA ** (-jnp.arange(n_f, dtype=F32) / n_f)
    ang = jnp.concatenate([r[:, None] * freqs, cidx[:, None] * freqs], axis=-1)
    return jnp.cos(ang).astype(F32), jnp.sin(ang).astype(F32)


def _rope_tables(rot_dim, lead, trail, reps):
    cos, sin = _axial_angles(rot_dim)
    z = jnp.zeros_like(sin)
    one_l, zero_l = jnp.ones((DEC_SEQ, lead), F32), jnp.zeros((DEC_SEQ, lead), F32)
    one_t, zero_t = jnp.ones((DEC_SEQ, trail), F32), jnp.zeros((DEC_SEQ, trail), F32)
    c = jnp.concatenate([one_l, cos, cos, one_t], axis=1)
    sa = jnp.concatenate([zero_l, z, sin, zero_t], axis=1)
    sb = jnp.concatenate([zero_l, -sin, z, zero_t], axis=1)
    return tuple(jnp.tile(t, (1, reps)) for t in (c, sa, sb))


def _table_specs(tm):
    spec = pl.BlockSpec((tm, LANE), lambda t: (t % (DEC_SEQ // tm), 0))
    return [spec, spec, spec]


def _group_mean_sq(x, g_ref, n):
    x2 = x * x
    hi = x2.astype(BF16)
    lo = (x2 - hi.astype(F32)).astype(BF16)
    g = g_ref[:n, :n]
    return _dot(hi, g) + _dot(lo, g)


def _gqa_proj_kernel(*refs, rope, f32_out):
    y_ref, g_ref, sh_ref, sc_ref, w_ref, grp_ref, gq_ref, gk_ref = refs[:8]
    refs = refs[8:]
    if rope:
        c, sa, sb = (r[...] for r in refs[:3])
        refs = refs[3:]
    q_ref, kd_ref, vd_ref = refs[:3]
    h = _rms_mod(y_ref[...], g_ref[...], sh_ref[...], sc_ref[...]).astype(BF16)
    qkv = _dot(h, w_ref[...])
    nq, nk = H_B * HD_B, KV_B * HD_B
    q, k, v = qkv[:, :nq], qkv[:, nq:nq + nk], qkv[:, nq + nk:]
    qn = q * lax.rsqrt(_group_mean_sq(q, grp_ref, nq) + EPS) * gq_ref[...]
    kn = k * lax.rsqrt(_group_mean_sq(k, grp_ref, nk) + EPS) * gk_ref[...]
    if f32_out:
        kf_ref, vf_ref = refs[3:5]
        kf_ref[...] = kn
        vf_ref[...] = v
    lo_half = lax.broadcasted_iota(jnp.int32, (q.shape[0], LANE), 1) < HD_B
    scale = HD_B ** -0.5
    for b in range(nq // LANE):
        xb = qn[:, b * LANE:(b + 1) * LANE]
        if rope:
            xb = _rope_block(xb, c, sa, sb, HD_B // 2)
        q_ref[:, b * LANE:(b + 1) * LANE] = (xb * scale).astype(BF16)
    for b in range(nk // LANE):
        kb = kn[:, b * LANE:(b + 1) * LANE]
        vb = v[:, b * LANE:(b + 1) * LANE]
        if rope:
            kb = _rope_block(kb, c, sa, sb, HD_B // 2)
        for x, dst in ((kb, kd_ref), (vb, vd_ref)):
            sw = pltpu.roll(x, HD_B, 1)
            dst[:, 2 * b * LANE:(2 * b + 1) * LANE] = jnp.where(lo_half, x, sw).astype(BF16)
            dst[:, (2 * b + 1) * LANE:(2 * b + 2) * LANE] = jnp.where(lo_half, sw, x).astype(BF16)


def _gqa_proj(y, norm_g, mod, layer, w, grp, gq, gk, tables, latent):
    n_tok = N_LAT if latent else N_CTX
    t0 = N_CTX // TM if latent else 0
    nq, nk = H_B * HD_B, KV_B * HD_B
    in_specs = [
        pl.BlockSpec((TM, D_MODEL), lambda t: (t + t0, 0)),
        pl.BlockSpec((None, 1, D_MODEL), lambda t: (layer, 0, 0)),
        _mod_spec(layer, 3, TM, t0),
        _mod_spec(layer, 4, TM, t0),
        _resident(w.shape),
        _resident(grp.shape),
        _row_spec(nq),
        _row_spec(nk),
    ]
    args = [y, norm_g.reshape(DEPTH, 1, D_MODEL), mod, mod, w, grp, gq, gk]
    out_shape = [jax.ShapeDtypeStruct((n_tok, nq), BF16),
                 jax.ShapeDtypeStruct((n_tok, 2 * nk), BF16),
                 jax.ShapeDtypeStruct((n_tok, 2 * nk), BF16)]
    out_specs = [pl.BlockSpec((TM, nq), lambda t: (t, 0)),
                 pl.BlockSpec((TM, 2 * nk), lambda t: (t, 0)),
                 pl.BlockSpec((TM, 2 * nk), lambda t: (t, 0))]
    if latent:
        in_specs += _table_specs(TM)
        args += list(tables)
    else:
        out_shape += [jax.ShapeDtypeStruct((n_tok, nk), F32)] * 2
        out_specs += [pl.BlockSpec((TM, nk), lambda t: (t, 0))] * 2
    return pl.pallas_call(
        functools.partial(_gqa_proj_kernel, rope=latent, f32_out=not latent),
        grid=(n_tok // TM,),
        in_specs=in_specs,
        out_specs=out_specs,
        out_shape=out_shape,
        compiler_params=_params("parallel"),
        name="gqa_proj_lat" if latent else "gqa_proj_ctx",
    )(*args)


def _diff_proj_kernel(*refs, rope, f32_out):
    y_ref, g_ref, sh_ref, sc_ref, w_ref = refs[:5]
    refs = refs[5:]
    if rope:
        c, sa, sb = (r[...] for r in refs[:3])
        refs = refs[3:]
    q_ref, k_ref, v_ref = refs[:3]
    h = _rms_mod(y_ref[...], g_ref[...], sh_ref[...], sc_ref[...]).astype(BF16)
    qkv = _dot(h, w_ref[...])
    nq = H_C * 2 * DK_C
    q, k, v = qkv[:, :nq], qkv[:, nq:2 * nq], qkv[:, 2 * nq:]
    if f32_out:
        kf_ref, vf_ref = refs[3:5]
        kf_ref[...] = k
        vf_ref[...] = v
    v_ref[...] = v.astype(BF16)
    scale = DK_C ** -0.5
    for b in range(nq // LANE):
        qb = q[:, b * LANE:(b + 1) * LANE]
        kb = k[:, b * LANE:(b + 1) * LANE]
        if rope:
            qb = _rope_block(qb, c, sa, sb, DK_C // 2)
            kb = _rope_block(kb, c, sa, sb, DK_C // 2)
        q_ref[:, b * LANE:(b + 1) * LANE] = (qb * scale).astype(BF16)
        k_ref[:, b * LANE:(b + 1) * LANE] = kb.astype(BF16)


def _diff_proj(y, norm_g, mod, layer, w, tables, latent):
    n_tok = N_LAT if latent else N_CTX
    t0 = N_CTX // TM if latent else 0
    nq, nv = H_C * 2 * DK_C, H_C * DV_C
    in_specs = [
        pl.BlockSpec((TM, D_MODEL), lambda t: (t + t0, 0)),
        pl.BlockSpec((None, 1, D_MODEL), lambda t: (layer, 0, 0)),
        _mod_spec(layer, 3, TM, t0),
        _mod_spec(layer, 4, TM, t0),
        _resident(w.shape),
    ]
    args = [y, norm_g.reshape(DEPTH, 1, D_MODEL), mod, mod, w]
    out_shape = [jax.ShapeDtypeStruct((n_tok, nq), BF16),
                 jax.ShapeDtypeStruct((n_tok, nq), BF16),
                 jax.ShapeDtypeStruct((n_tok, nv), BF16)]
    out_specs = [pl.BlockSpec((TM, nq), lambda t: (t, 0)),
                 pl.BlockSpec((TM, nq), lambda t: (t, 0)),
                 pl.BlockSpec((TM, nv), lambda t: (t, 0))]
    if latent:
        in_specs += _table_specs(TM)
        args += list(tables)
    else:
        out_shape += [jax.ShapeDtypeStruct((n_tok, nq), F32), jax.ShapeDtypeStruct((n_tok, nv), F32)]
        out_specs += [pl.BlockSpec((TM, nq), lambda t: (t, 0)), pl.BlockSpec((TM, nv), lambda t: (t, 0))]
    return pl.pallas_call(
        functools.partial(_diff_proj_kernel, rope=latent, f32_out=not latent),
        grid=(n_tok // TM,),
        in_specs=in_specs,
        out_specs=out_specs,
        out_shape=out_shape,
        compiler_params=_params("parallel"),
        name="diff_proj_lat" if latent else "diff_proj_ctx",
    )(*args)


MLA_DOWN = 768
MLA_KR_BLOCK = 640
MLA_HEAD = LANE


def _mla_proj_kernel(*refs, rope, f32_out):
    y_ref, g_ref, sh_ref, sc_ref, wd_ref, gq_ref, gkv_ref, wuq_ref, wuk_ref, wuv_ref = refs[:10]
    refs = refs[10:]
    if rope:
        c, sa, sb = (r[...] for r in refs[:3])
        refs = refs[3:]
    q_ref, k_ref, v_ref = refs[:3]
    h = _rms_mod(y_ref[...], g_ref[...], sh_ref[...], sc_ref[...]).astype(BF16)
    down = _dot(h, wd_ref[...])
    cq = _rms(down[:, :Q_LORA], gq_ref[...])
    ckv = _rms(down[:, Q_LORA:Q_LORA + KV_LORA], gkv_ref[...])
    krb = down[:, MLA_KR_BLOCK:MLA_KR_BLOCK + LANE]
    if f32_out:
        ckv_ref, kr_ref = refs[3:5]
        ckv_ref[...] = ckv
        kr_ref[...] = krb
    if rope:
        krb = _rope_block(krb, c, sa, sb, ROPE_A // 2)
    cq16, ckv16 = cq.astype(BF16), ckv.astype(BF16)
    q = _dot(cq16, wuq_ref[...])
    kn = _dot(ckv16, wuk_ref[...])
    v_ref[...] = _dot(ckv16, wuv_ref[...]).astype(BF16)
    scale = (NOPE_A + ROPE_A) ** -0.5
    for hd in range(H_A):
        qh = q[:, hd * MLA_HEAD:(hd + 1) * MLA_HEAD]
        if rope:
            qh = _rope_block(qh, c, sa, sb, ROPE_A // 2)
        q_ref[:, hd * MLA_HEAD:(hd + 1) * MLA_HEAD] = (qh * scale).astype(BF16)
        k_ref[:, hd * MLA_HEAD:(hd + 1) * MLA_HEAD] = (kn[:, hd * MLA_HEAD:(hd + 1) * MLA_HEAD] + krb).astype(BF16)


def _mla_proj(y, norm_g, mod, layer, wd, gq, gkv, wuq, wuk, wuv, tables, latent):
    n_tok = N_LAT if latent else N_CTX
    t0 = N_CTX // TM if latent else 0
    nqk, nv = H_A * MLA_HEAD, H_A * V_A
    in_specs = [
        pl.BlockSpec((TM, D_MODEL), lambda t: (t + t0, 0)),
        pl.BlockSpec((None, 1, D_MODEL), lambda t: (layer, 0, 0)),
        _mod_spec(layer, 3, TM, t0),
        _mod_spec(layer, 4, TM, t0),
        _resident(wd.shape), _row_spec(Q_LORA), _row_spec(KV_LORA),
        _resident(wuq.shape), _resident(wuk.shape), _resident(wuv.shape),
    ]
    args = [y, norm_g.reshape(DEPTH, 1, D_MODEL), mod, mod, wd, gq, gkv, wuq, wuk, wuv]
    out_shape = [jax.ShapeDtypeStruct((n_tok, nqk), BF16),
                 jax.ShapeDtypeStruct((n_tok, nqk), BF16),
                 jax.ShapeDtypeStruct((n_tok, nv), BF16)]
    out_specs = [pl.BlockSpec((TM, nqk), lambda t: (t, 0)),
                 pl.BlockSpec((TM, nqk), lambda t: (t, 0)),
                 pl.BlockSpec((TM, nv), lambda t: (t, 0))]
    if latent:
        in_specs += _table_specs(TM)
        args += list(tables)
    else:
        out_shape += [jax.ShapeDtypeStruct((n_tok, KV_LORA), F32), jax.ShapeDtypeStruct((n_tok, LANE), F32)]
        out_specs += [pl.BlockSpec((TM, KV_LORA), lambda t: (t, 0)), pl.BlockSpec((TM, LANE), lambda t: (t, 0))]
    return pl.pallas_call(
        functools.partial(_mla_proj_kernel, rope=latent, f32_out=not latent),
        grid=(n_tok // TM,),
        in_specs=in_specs,
        out_specs=out_specs,
        out_shape=out_shape,
        compiler_params=_params("parallel"),
        name="mla_proj_lat" if latent else "mla_proj_ctx",
    )(*args)


def _mla_cache_kernel(ckv_ref, krb_ref, wuk_ref, wuv_ref, k_ref, v_ref):
    ckv16 = ckv_ref[...].astype(BF16)
    kn = _dot(ckv16, wuk_ref[...])
    v_ref[...] = _dot(ckv16, wuv_ref[...]).astype(BF16)
    krb = krb_ref[...]
    for hd in range(H_A):
        k_ref[:, hd * MLA_HEAD:(hd + 1) * MLA_HEAD] = (kn[:, hd * MLA_HEAD:(hd + 1) * MLA_HEAD] + krb).astype(BF16)


def _mla_cache_kv(ckv, krb, wuk, wuv):
    n = ckv.shape[0]
    nqk, nv = H_A * MLA_HEAD, H_A * V_A
    return pl.pallas_call(
        _mla_cache_kernel,
        grid=(n // TM,),
        in_specs=[pl.BlockSpec((TM, KV_LORA), lambda t: (t, 0)),
                  pl.BlockSpec((TM, LANE), lambda t: (t, 0)),
                  _resident(wuk.shape), _resident(wuv.shape)],
        out_specs=[pl.BlockSpec((TM, nqk), lambda t: (t, 0)), pl.BlockSpec((TM, nv), lambda t: (t, 0))],
        out_shape=[jax.ShapeDtypeStruct((n, nqk), BF16), jax.ShapeDtypeStruct((n, nv), BF16)],
        compiler_params=_params("parallel"),
        name="mla_cache_kv",
    )(ckv, krb, wuk, wuv)


def _softmax_parts(s):
    m = jnp.max(s, axis=-1, keepdims=True)
    e = jnp.exp(s - m)
    return e, jnp.sum(e, axis=-1, keepdims=True)


def _attn_pair_kernel(qa_ref, qb_ref, ka_ref, kb_ref, v_ref, o_ref, *, mask_q):
    lo_half = lax.broadcasted_iota(jnp.int32, qa_ref.shape, 1) < LANE // 2
    qa, qb = qa_ref[...], qb_ref[...]
    if mask_q:
        qa = jnp.where(lo_half, qa, jnp.zeros_like(qa))
        qb = jnp.where(lo_half, jnp.zeros_like(qb), qb)

    def head(q, k_ref):
        e, l = _softmax_parts(_dot_nt(q, k_ref[...]))
        return _dot(e.astype(BF16), v_ref[...]) / l

    o_ref[...] = jnp.where(lo_half, head(qa, ka_ref), head(qb, kb_ref)).astype(BF16)


def _attn_pairs(q, k, v, qa_blk, qb_blk, ka_blk, kb_blk, v_blk, mask_q, name):
    nb, sq, _ = q.shape
    sk = k.shape[1]
    tq = min(TQ, sq)
    n_pairs = D_MODEL // LANE

    def qspec(blk):
        return pl.BlockSpec((None, tq, LANE), lambda b, p, i: (b, i, blk(p)))

    def kspec(blk):
        return pl.BlockSpec((None, sk, LANE), lambda b, p, i: (b, 0, blk(p)))

    return pl.pallas_call(
        functools.partial(_attn_pair_kernel, mask_q=mask_q),
        grid=(nb, n_pairs, sq // tq),
        in_specs=[qspec(qa_blk), qspec(qb_blk), kspec(ka_blk), kspec(kb_blk), kspec(v_blk)],
        out_specs=pl.BlockSpec((None, tq, LANE), lambda b, p, i: (b, i, p)),
        out_shape=jax.ShapeDtypeStruct((nb, sq, D_MODEL), BF16),
        compiler_params=_params("parallel", "parallel", "arbitrary"),
        name=name,
    )(q, q, k, k, v)


def _attn_diff_kernel(q_ref, k_ref, v_ref, lq1_ref, lk1_ref, lq2_ref, lk2_ref, gsub_ref, o_ref, *, lam_init):
    lo_half = lax.broadcasted_iota(jnp.int32, q_ref.shape, 1) < LANE // 2
    q = q_ref[...]
    k = k_ref[...]
    e0, l0 = _softmax_parts(_dot_nt(jnp.where(lo_half, q, jnp.zeros_like(q)), k))
    e1, l1 = _softmax_parts(_dot_nt(jnp.where(lo_half, jnp.zeros_like(q), q), k))
    lam = (jnp.exp(jnp.sum(lq1_ref[...] * lk1_ref[...], axis=-1, keepdims=True))
           - jnp.exp(jnp.sum(lq2_ref[...] * lk2_ref[...], axis=-1, keepdims=True)) + lam_init)
    a = e0 * (1.0 / l0) - e1 * (lam / l1)
    o = _dot(a.astype(BF16), v_ref[...])
    o_ref[...] = (_rms(o, gsub_ref[...]) * (1.0 - lam_init)).astype(BF16)


def _attn_diff(q, k, v, lq1, lk1, lq2, lk2, gsub, lam_init, name):
    nb, sq, _ = q.shape
    sk = k.shape[1]
    tq = min(TQ, sq)
    vec = _row_spec(DK_C)
    return pl.pallas_call(
        functools.partial(_attn_diff_kernel, lam_init=lam_init),
        grid=(nb, H_C, sq // tq),
        in_specs=[pl.BlockSpec((None, tq, LANE), lambda b, h, i: (b, i, h)),
                  pl.BlockSpec((None, sk, LANE), lambda b, h, i: (b, 0, h)),
                  pl.BlockSpec((None, sk, LANE), lambda b, h, i: (b, 0, h)),
                  vec, vec, vec, vec, _row_spec(DV_C)],
        out_specs=pl.BlockSpec((None, tq, LANE), lambda b, h, i: (b, i, h)),
        out_shape=jax.ShapeDtypeStruct((nb, sq, D_MODEL), BF16),
        compiler_params=_params("parallel", "parallel", "arbitrary"),
        name=name,
    )(q, k, v, lq1, lk1, lq2, lk2, gsub)


def _oproj_kernel(o_ref, w_ref, y_ref, gate_ref, out_ref):
    out_ref[...] = y_ref[...] + gate_ref[...] * _dot(o_ref[...], w_ref[...])


def _oproj(o, w, y, mod, layer):
    tok = pl.BlockSpec((TM, D_MODEL), lambda t: (t, 0))
    return pl.pallas_call(
        _oproj_kernel,
        grid=(N_TOK // TM,),
        in_specs=[tok, _resident(w.shape), tok, _mod_spec(layer, 5, TM, 0)],
        out_specs=tok,
        out_shape=jax.ShapeDtypeStruct((N_TOK, D_MODEL), F32),
        compiler_params=_params("parallel"),
        name="oproj",
    )(o, w, y, mod)


def _final_kernel(y_ref, g_ref, o_ref):
    o_ref[...] = _rms(y_ref[...], g_ref[...])


def _final_norm(y, g, n_tok, t0):
    return pl.pallas_call(
        _final_kernel,
        grid=(n_tok // TM,),
        in_specs=[pl.BlockSpec((TM, D_MODEL), lambda t: (t + t0, 0)), _row_spec(D_MODEL)],
        out_specs=pl.BlockSpec((TM, D_MODEL), lambda t: (t, 0)),
        out_shape=jax.ShapeDtypeStruct((n_tok, D_MODEL), F32),
        compiler_params=_params("parallel"),
        name="final_norm",
    )(y, g)


def _pad_heads(w, n_heads, width):
    k = w.shape[0]
    w = w.reshape(k, n_heads, width)
    return jnp.pad(w, ((0, 0), (0, 0), (0, MLA_HEAD - width))).reshape(k, n_heads * MLA_HEAD)


def _mla_down_layout(w):
    z = lambda n: jnp.zeros((D_MODEL, n), w.dtype)
    return jnp.concatenate([w[:, :Q_LORA + KV_LORA], z(NOPE_A), w[:, Q_LORA + KV_LORA:],
                            z(LANE - NOPE_A - ROPE_A)], axis=1)


def kernel(x_prompt, x_sample, cache_mla_ckv, cache_mla_kr, cache_gqa_k, cache_gqa_v, cache_diff_k, cache_diff_v, c, c_ctx, ada_w, ada_b, norm_ffn1, norm_mix, norm_ffn2, ffn1_w_in, ffn1_w_out, ffn2_w_in, ffn2_w_out, mla_w_down, mla_g_q, mla_g_kv, mla_w_uq, mla_w_uk, mla_w_uv, mla_w_o, gqa_w_qkv, gqa_g_q, gqa_g_k, gqa_w_o, diff_w_qkv, diff_lq1, diff_lk1, diff_lq2, diff_lk2, diff_g_sub, diff_w_o, norm_final):
    y = jnp.concatenate([x_prompt.reshape(N_CTX, D_MODEL), x_sample.reshape(N_LAT, D_MODEL)], axis=0)
    c_rows = jnp.concatenate([c_ctx[None, :], c, jnp.zeros((MOD_ROWS - 1 - DEC_BATCH, D_MODEL), F32)], axis=0)
    mod = _ada(c_rows, ada_w, ada_b)

    w1_in, w1_out = ffn1_w_in.astype(BF16), ffn1_w_out.astype(BF16)
    w2_in, w2_out = ffn2_w_in.astype(BF16), ffn2_w_out.astype(BF16)
    tab_a = _rope_tables(ROPE_A, NOPE_A, LANE - NOPE_A - ROPE_A, 1)
    tab_64 = _rope_tables(HD_B, 0, 0, 2)
    idx = jnp.arange(H_B * HD_B) // HD_B
    grp = jnp.where(idx[:, None] == idx[None, :], 1.0 / HD_B, 0.0).astype(BF16)

    new_ckv, new_kr, new_gk, new_gv, new_dk, new_dv = [], [], [], [], [], []
    for i in range(DEPTH):
        kind, j = i % N_MIXERS, i // N_MIXERS
        y = _ffn(y, norm_ffn1, mod, i, 0, w1_in, w1_out)
        if kind == 0:
            wd = _mla_down_layout(mla_w_down[j]).astype(BF16)
            wuq = _pad_heads(mla_w_uq[j], H_A, NOPE_A + ROPE_A).astype(BF16)
            wuk = _pad_heads(mla_w_uk[j], H_A, NOPE_A).astype(BF16)
            wuv = mla_w_uv[j].astype(BF16)
            gq, gkv = mla_g_q[j][None, :], mla_g_kv[j][None, :]
            qc, kc, vc, ckv, krb = _mla_proj(y, norm_mix, mod, i, wd, gq, gkv, wuq, wuk, wuv, None, False)
            new_ckv.append(ckv.reshape(BATCH, SEQ, KV_LORA))
            new_kr.append(krb[:, NOPE_A:NOPE_A + ROPE_A].reshape(BATCH, SEQ, ROPE_A))
            ql, kl, vl = _mla_proj(y, norm_mix, mod, i, wd, gq, gkv, wuq, wuk, wuv, tab_a, True)
            kr_pad = jnp.pad(cache_mla_kr[:, j].reshape(DEC_BATCH * PAST_LEN, ROPE_A),
                             ((0, 0), (NOPE_A, LANE - NOPE_A - ROPE_A)))
            kp, vp = _mla_cache_kv(cache_mla_ckv[:, j].reshape(DEC_BATCH * PAST_LEN, KV_LORA), kr_pad, wuk, wuv)
            blocks = (lambda p: 2 * p, lambda p: 2 * p + 1, lambda p: 2 * p, lambda p: 2 * p + 1, lambda p: p)
            mask_q = False
            w_o = mla_w_o[j]
        elif kind == 1:
            w = gqa_w_qkv[j].astype(BF16)
            gq = jnp.tile(gqa_g_q[j], H_B)[None, :]
            gk = jnp.tile(gqa_g_k[j], KV_B)[None, :]
            qc, kc, vc, kf, vf = _gqa_proj(y, norm_mix, mod, i, w, grp, gq, gk, None, False)
            new_gk.append(kf.reshape(BATCH, SEQ, KV_B, HD_B))
            new_gv.append(vf.reshape(BATCH, SEQ, KV_B, HD_B))
            ql, kl, vl = _gqa_proj(y, norm_mix, mod, i, w, grp, gq, gk, tab_64, True)

            def dup(x):
                x = x.reshape(DEC_BATCH * PAST_LEN, KV_B, 1, HD_B)
                return jnp.broadcast_to(x, (DEC_BATCH * PAST_LEN, KV_B, 2, HD_B)).reshape(
                    DEC_BATCH * PAST_LEN, 2 * KV_B * HD_B).astype(BF16)

            kp, vp = dup(cache_gqa_k[:, j]), dup(cache_gqa_v[:, j])
            blocks = (lambda p: p, lambda p: p, lambda p: p // 2, lambda p: p // 2, lambda p: p // 2)
            mask_q = True
            w_o = gqa_w_o[j]
        else:
            w = diff_w_qkv[j].astype(BF16)
            qc, kc, vc, kf, vf = _diff_proj(y, norm_mix, mod, i, w, None, False)
            new_dk.append(kf.reshape(BATCH, SEQ, H_C, 2, DK_C))
            new_dv.append(vf.reshape(BATCH, SEQ, H_C, DV_C))
            ql, kl, vl = _diff_proj(y, norm_mix, mod, i, w, tab_64, True)
            kp = cache_diff_k[:, j].reshape(DEC_BATCH * PAST_LEN, H_C * 2 * DK_C).astype(BF16)
            vp = cache_diff_v[:, j].reshape(DEC_BATCH * PAST_LEN, H_C * DV_C).astype(BF16)
            w_o = diff_w_o[j]

        def b3(x, nb):
            return x.reshape(nb, x.shape[0] // nb, x.shape[1])

        k_all = jnp.concatenate([b3(kl, DEC_BATCH), b3(kp, DEC_BATCH)], axis=1)
        v_all = jnp.concatenate([b3(vl, DEC_BATCH), b3(vp, DEC_BATCH)], axis=1)
        if kind == 2:
            lam_init = 0.8 - 0.6 * math.exp(-0.3 * i)
            lvec = (diff_lq1[j][None, :], diff_lk1[j][None, :], diff_lq2[j][None, :], diff_lk2[j][None, :],
                    diff_g_sub[j][None, :])
            o_ctx = _attn_diff(b3(qc, BATCH), b3(kc, BATCH), b3(vc, BATCH), *lvec, lam_init, "attn_diff_ctx")
            o_lat = _attn_diff(b3(ql, DEC_BATCH), k_all, v_all, *lvec, lam_init, "attn_diff_lat")
        else:
            o_ctx = _attn_pairs(b3(qc, BATCH), b3(kc, BATCH), b3(vc, BATCH), *blocks, mask_q, "attn_ctx")
            o_lat = _attn_pairs(b3(ql, DEC_BATCH), k_all, v_all, *blocks, mask_q, "attn_lat")
        o = jnp.concatenate([o_ctx.reshape(N_CTX, D_MODEL), o_lat.reshape(N_LAT, D_MODEL)], axis=0)
        y = _oproj(o, w_o.astype(BF16), y, mod, i)
        y = _ffn(y, norm_ffn2, mod, i, 6, w2_in, w2_out)

    g_final = norm_final[None, :]
    y_prompt = _final_norm(y, g_final, N_CTX, 0).reshape(BATCH, SEQ, D_MODEL)
    y_sample = _final_norm(y, g_final, N_LAT, N_CTX // TM).reshape(DEC_BATCH, DEC_SEQ, D_MODEL)
    return (y_prompt, y_sample,
            jnp.stack(new_ckv, axis=1), jnp.stack(new_kr, axis=1),
            jnp.stack(new_gk, axis=1), jnp.stack(new_gv, axis=1),
            jnp.stack(new_dk, axis=1), jnp.stack(new_dv, axis=1))
```

```python
import functools
import math

import jax
import jax.numpy as jnp
from jax import lax
from jax.experimental import pallas as pl
from jax.experimental.pallas import tpu as pltpu

D_MODEL = 1024
BATCH = 32
SEQ = 256
DEPTH = 4
DEC_BATCH = 2
DEC_SEQ = 4096
PAST_LEN = 512
GRID_W = 64
N_MIXERS = 3
ROPE_THETA = 10000.0
EPS = 1e-6
N_MOD = 9
D_FF = 2816
H_A = 16
Q_LORA = 384
KV_LORA = 256
NOPE_A = 64
ROPE_A = 32
V_A = 64
H_B = 16
KV_B = 4
HD_B = 64
H_C = 8
DK_C = 64
DV_C = 128

N_CTX = BATCH * SEQ
N_LAT = DEC_BATCH * DEC_SEQ
N_TOK = N_CTX + N_LAT
LANE = 128
MOD_ROWS = 8
VMEM_LIMIT = 56 * 1024 * 1024
F32 = jnp.float32
BF16 = jnp.bfloat16

TM = 512
TQ = 256
FF_CHUNK = 1408
KV_CHUNK = 512
LOG2_E = math.log2(math.e)


def _params(*sem):
    return pltpu.CompilerParams(dimension_semantics=sem, vmem_limit_bytes=VMEM_LIMIT)


def _mod_spec(layer, k, tm, t0):
    tiles_ctx = N_CTX // tm
    tiles_per_lat = DEC_SEQ // tm

    def imap(t, *_):
        tg = t + t0
        row = jnp.where(tg < tiles_ctx, 0, 1 + (tg - tiles_ctx) // tiles_per_lat)
        return (layer, k, row, 0, 0)

    return pl.BlockSpec((None, None, None, 1, D_MODEL), imap)


def _row_spec(n):
    return pl.BlockSpec((1, n), lambda *_: (0, 0))


def _resident(shape):
    return pl.BlockSpec(shape, lambda *_: (0,) * len(shape))


def _layer_resident(shape, layer):
    return pl.BlockSpec((None,) + shape, lambda *_: (layer,) + (0,) * len(shape))


def _rms_mod(x, g, shift, scale):
    ms = jnp.mean(x * x, axis=-1, keepdims=True)
    return (x * lax.rsqrt(ms + EPS) * g) * (1.0 + scale) + shift


def _rms(x, g):
    ms = jnp.mean(x * x, axis=-1, keepdims=True)
    return x * lax.rsqrt(ms + EPS) * g


def _dot(a, b):
    return jnp.dot(a, b, preferred_element_type=F32)


def _dot_nt(a, b):
    return lax.dot_general(a, b, (((1,), (1,)), ((), ())), preferred_element_type=F32)


def _ada_kernel(c_ref, w_ref, b_ref, o_ref):
    c = c_ref[...]
    act = c * jax.nn.sigmoid(c)
    o_ref[...] = jnp.dot(act, w_ref[...], precision=lax.Precision.HIGHEST,
                         preferred_element_type=F32) + b_ref[...]


def _ada(c_rows, ada_w, ada_b):
    out = pl.pallas_call(
        _ada_kernel,
        grid=(DEPTH, N_MOD),
        in_specs=[
            _resident((MOD_ROWS, D_MODEL)),
            pl.BlockSpec((None, D_MODEL, D_MODEL), lambda i, k: (i, 0, k)),
            pl.BlockSpec((None, None, 1, D_MODEL), lambda i, k: (i, k, 0, 0)),
        ],
        out_specs=pl.BlockSpec((None, None, MOD_ROWS, D_MODEL), lambda i, k: (i, k, 0, 0)),
        out_shape=jax.ShapeDtypeStruct((DEPTH, N_MOD, MOD_ROWS, D_MODEL), F32),
        compiler_params=_params("parallel", "parallel"),
        name="ada_mod",
    )(c_rows, ada_w, ada_b.reshape(DEPTH, N_MOD, 1, D_MODEL))
    return out.reshape(DEPTH, N_MOD, MOD_ROWS, 1, D_MODEL)


def _part_specs(tm, width):
    tiles_ctx = N_CTX // tm
    return [pl.BlockSpec((tm, width), lambda t: (jnp.minimum(t, tiles_ctx - 1), 0)),
            pl.BlockSpec((tm, width), lambda t: (jnp.maximum(t - tiles_ctx, 0), 0))]


def _pick_part(ctx_ref, lat_ref):
    return jnp.where(pl.program_id(0) < N_CTX // TM, ctx_ref[...], lat_ref[...])


def _ffn_kernel(*refs, split_in, oproj):
    refs = list(refs)
    x = _pick_part(refs.pop(0), refs.pop(0)) if split_in else refs.pop(0)[...]
    if oproj:
        o = _pick_part(refs.pop(0), refs.pop(0))
        wo_ref, og_ref = refs.pop(0), refs.pop(0)
        x = x + og_ref[...] * _dot(o, wo_ref[...])
    g_ref, sh_ref, sc_ref, gate_ref, win_ref, wout_ref, o_ref, a_ref = refs
    h = _rms_mod(x, g_ref[...], sh_ref[...], sc_ref[...]).astype(BF16)
    for c in range(D_FF // FF_CHUNK):
        lo = c * FF_CHUNK
        gt = _dot(h, win_ref[:, lo:lo + FF_CHUNK])
        up = _dot(h, win_ref[:, D_FF + lo:D_FF + lo + FF_CHUNK])
        a_ref[:, lo:lo + FF_CHUNK] = (gt * jax.nn.sigmoid(gt) * up).astype(BF16)
    f = _dot(a_ref[...], wout_ref[...])
    o_ref[...] = x + (0.5 * gate_ref[...]) * f


def _ffn(y_parts, norm_g, mod, layer, k0, w_in, w_out, attn=None):
    tok = pl.BlockSpec((TM, D_MODEL), lambda t: (t, 0))
    split_in = len(y_parts) == 2
    in_specs = _part_specs(TM, D_MODEL) if split_in else [tok]
    args = list(y_parts)
    if attn is not None:
        o_ctx, o_lat, w_o = attn
        in_specs += _part_specs(TM, D_MODEL) + [_resident(w_o.shape), _mod_spec(layer, 5, TM, 0)]
        args += [o_ctx, o_lat, w_o, mod]
    in_specs += [
        pl.BlockSpec((None, 1, D_MODEL), lambda t: (layer, 0, 0)),
        _mod_spec(layer, k0, TM, 0),
        _mod_spec(layer, k0 + 1, TM, 0),
        _mod_spec(layer, k0 + 2, TM, 0),
        pl.BlockSpec((None, D_MODEL, 2 * D_FF), lambda t: (layer, 0, 0), pipeline_mode=pl.Buffered(1)),
        pl.BlockSpec((None, D_FF, D_MODEL), lambda t: (layer, 0, 0), pipeline_mode=pl.Buffered(1)),
    ]
    args += [norm_g.reshape(DEPTH, 1, D_MODEL), mod, mod, mod, w_in, w_out]
    return pl.pallas_call(
        functools.partial(_ffn_kernel, split_in=split_in, oproj=attn is not None),
        grid=(N_TOK // TM,),
        in_specs=in_specs,
        out_specs=tok,
        out_shape=jax.ShapeDtypeStruct((N_TOK, D_MODEL), F32),
        scratch_shapes=[pltpu.VMEM((TM, D_FF), BF16)],
        compiler_params=_params("parallel"),
        name="ffn_oproj" if attn is not None else "ffn",
    )(*args)


def _rope_block(x, c, sa, sb, half):
    return x * c + pltpu.roll(x, half, 1) * sa + pltpu.roll(x, LANE - half, 1) * sb


def _axial_angles(rot_dim):
    pos = jnp.arange(DEC_SEQ, dtype=jnp.int32)
    r = (pos // GRID_W).astype(F32)
    cidx = (pos % GRID_W).astype(F32)
    n_f = rot_dim // 4
    freqs = ROPE_THETA ** (-jnp.arange(n_f, dtype=F32) / n_f)
    ang = jnp.concatenate([r[:, None] * freqs, cidx[:, None] * freqs], axis=-1)
    return jnp.cos(ang).astype(F32), jnp.sin(ang).astype(F32)


def _rope_tables(rot_dim, lead, trail, reps):
    cos, sin = _axial_angles(rot_dim)
    z = jnp.zeros_like(sin)
    one_l, zero_l = jnp.ones((DEC_SEQ, lead), F32), jnp.zeros((DEC_SEQ, lead), F32)
    one_t, zero_t = jnp.ones((DEC_SEQ, trail), F32), jnp.zeros((DEC_SEQ, trail), F32)
    c = jnp.concatenate([one_l, cos, cos, one_t], axis=1)
    sa = jnp.concatenate([zero_l, z, sin, zero_t], axis=1)
    sb = jnp.concatenate([zero_l, -sin, z, zero_t], axis=1)
    return tuple(jnp.tile(t, (1, reps)) for t in (c, sa, sb))


def _table_specs(tm):
    spec = pl.BlockSpec((tm, LANE), lambda t: (t % (DEC_SEQ // tm), 0))
    return [spec, spec, spec]


def _group_mean_sq(x, g_ref, n):
    x2 = x * x
    hi = x2.astype(BF16)
    lo = (x2 - hi.astype(F32)).astype(BF16)
    g = g_ref[:n, :n]
    return _dot(hi, g) + _dot(lo, g)


def _gqa_proj_kernel(*refs, rope, f32_out):
    y_ref, g_ref, sh_ref, sc_ref, w_ref, grp_ref, gq_ref, gk_ref = refs[:8]
    refs = refs[8:]
    if rope:
        c, sa, sb = (r[...] for r in refs[:3])
        refs = refs[3:]
    q_ref, kd_ref, vd_ref = refs[:3]
    h = _rms_mod(y_ref[...], g_ref[...], sh_ref[...], sc_ref[...]).astype(BF16)
    qkv = _dot(h, w_ref[...])
    nq, nk = H_B * HD_B, KV_B * HD_B
    q, k, v = qkv[:, :nq], qkv[:, nq:nq + nk], qkv[:, nq + nk:]
    qn = q * lax.rsqrt(_group_mean_sq(q, grp_ref, nq) + EPS) * gq_ref[...]
    kn = k * lax.rsqrt(_group_mean_sq(k, grp_ref, nk) + EPS) * gk_ref[...]
    if f32_out:
        kf_ref, vf_ref = refs[3:5]
        kf_ref[...] = kn
        vf_ref[...] = v
    lo_half = lax.broadcasted_iota(jnp.int32, (q.shape[0], LANE), 1) < HD_B
    scale = HD_B ** -0.5 * LOG2_E
    for b in range(nq // LANE):
        xb = qn[:, b * LANE:(b + 1) * LANE]
        if rope:
            xb = _rope_block(xb, c, sa, sb, HD_B // 2)
        q_ref[:, b * LANE:(b + 1) * LANE] = (xb * scale).astype(BF16)
    for b in range(nk // LANE):
        kb = kn[:, b * LANE:(b + 1) * LANE]
        vb = v[:, b * LANE:(b + 1) * LANE]
        if rope:
            kb = _rope_block(kb, c, sa, sb, HD_B // 2)
        for x, dst in ((kb, kd_ref), (vb, vd_ref)):
            sw = pltpu.roll(x, HD_B, 1)
            dst[:, 2 * b * LANE:(2 * b + 1) * LANE] = jnp.where(lo_half, x, sw).astype(BF16)
            dst[:, (2 * b + 1) * LANE:(2 * b + 2) * LANE] = jnp.where(lo_half, sw, x).astype(BF16)


def _gqa_proj(y, norm_g, mod, layer, w, grp, gq, gk, tables, latent):
    n_tok = N_LAT if latent else N_CTX
    t0 = N_CTX // TM if latent else 0
    nq, nk = H_B * HD_B, KV_B * HD_B
    in_specs = [
        pl.BlockSpec((TM, D_MODEL), lambda t: (t + t0, 0)),
        pl.BlockSpec((None, 1, D_MODEL), lambda t: (layer, 0, 0)),
        _mod_spec(layer, 3, TM, t0),
        _mod_spec(layer, 4, TM, t0),
        _resident(w.shape),
        _resident(grp.shape),
        _row_spec(nq),
        _row_spec(nk),
    ]
    args = [y, norm_g.reshape(DEPTH, 1, D_MODEL), mod, mod, w, grp, gq, gk]
    out_shape = [jax.ShapeDtypeStruct((n_tok, nq), BF16),
                 jax.ShapeDtypeStruct((n_tok, 2 * nk), BF16),
                 jax.ShapeDtypeStruct((n_tok, 2 * nk), BF16)]
    out_specs = [pl.BlockSpec((TM, nq), lambda t: (t, 0)),
                 pl.BlockSpec((TM, 2 * nk), lambda t: (t, 0)),
                 pl.BlockSpec((TM, 2 * nk), lambda t: (t, 0))]
    if latent:
        in_specs += _table_specs(TM)
        args += list(tables)
    else:
        out_shape += [jax.ShapeDtypeStruct((n_tok, nk), F32)] * 2
        out_specs += [pl.BlockSpec((TM, nk), lambda t: (t, 0))] * 2
    return pl.pallas_call(
        functools.partial(_gqa_proj_kernel, rope=latent, f32_out=not latent),
        grid=(n_tok // TM,),
        in_specs=in_specs,
        out_specs=out_specs,
        out_shape=out_shape,
        compiler_params=_params("parallel"),
        name="gqa_proj_lat" if latent else "gqa_proj_ctx",
    )(*args)


def _diff_proj_kernel(*refs, rope, f32_out):
    y_ref, g_ref, sh_ref, sc_ref, w_ref = refs[:5]
    refs = refs[5:]
    if rope:
        c, sa, sb = (r[...] for r in refs[:3])
        refs = refs[3:]
    q_ref, k_ref, v_ref = refs[:3]
    h = _rms_mod(y_ref[...], g_ref[...], sh_ref[...], sc_ref[...]).astype(BF16)
    qkv = _dot(h, w_ref[...])
    nq = H_C * 2 * DK_C
    q, k, v = qkv[:, :nq], qkv[:, nq:2 * nq], qkv[:, 2 * nq:]
    if f32_out:
        kf_ref, vf_ref = refs[3:5]
        kf_ref[...] = k
        vf_ref[...] = v
    v_ref[...] = v.astype(BF16)
    scale = DK_C ** -0.5 * LOG2_E
    for b in range(nq // LANE):
        qb = q[:, b * LANE:(b + 1) * LANE]
        kb = k[:, b * LANE:(b + 1) * LANE]
        if rope:
            qb = _rope_block(qb, c, sa, sb, DK_C // 2)
            kb = _rope_block(kb, c, sa, sb, DK_C // 2)
        q_ref[:, b * LANE:(b + 1) * LANE] = (qb * scale).astype(BF16)
        k_ref[:, b * LANE:(b + 1) * LANE] = kb.astype(BF16)


def _diff_proj(y, norm_g, mod, layer, w, tables, latent):
    n_tok = N_LAT if latent else N_CTX
    t0 = N_CTX // TM if latent else 0
    nq, nv = H_C * 2 * DK_C, H_C * DV_C
    in_specs = [
        pl.BlockSpec((TM, D_MODEL), lambda t: (t + t0, 0)),
        pl.BlockSpec((None, 1, D_MODEL), lambda t: (layer, 0, 0)),
        _mod_spec(layer, 3, TM, t0),
        _mod_spec(layer, 4, TM, t0),
        _resident(w.shape),
    ]
    args = [y, norm_g.reshape(DEPTH, 1, D_MODEL), mod, mod, w]
    out_shape = [jax.ShapeDtypeStruct((n_tok, nq), BF16),
                 jax.ShapeDtypeStruct((n_tok, nq), BF16),
                 jax.ShapeDtypeStruct((n_tok, nv), BF16)]
    out_specs = [pl.BlockSpec((TM, nq), lambda t: (t, 0)),
                 pl.BlockSpec((TM, nq), lambda t: (t, 0)),
                 pl.BlockSpec((TM, nv), lambda t: (t, 0))]
    if latent:
        in_specs += _table_specs(TM)
        args += list(tables)
    else:
        out_shape += [jax.ShapeDtypeStruct((n_tok, nq), F32), jax.ShapeDtypeStruct((n_tok, nv), F32)]
        out_specs += [pl.BlockSpec((TM, nq), lambda t: (t, 0)), pl.BlockSpec((TM, nv), lambda t: (t, 0))]
    return pl.pallas_call(
        functools.partial(_diff_proj_kernel, rope=latent, f32_out=not latent),
        grid=(n_tok // TM,),
        in_specs=in_specs,
        out_specs=out_specs,
        out_shape=out_shape,
        compiler_params=_params("parallel"),
        name="diff_proj_lat" if latent else "diff_proj_ctx",
    )(*args)


MLA_DOWN = 768
MLA_KR_BLOCK = 640
MLA_HEAD = LANE


def _mla_proj_kernel(*refs, rope, f32_out):
    y_ref, g_ref, sh_ref, sc_ref, wd_ref, gq_ref, gkv_ref, wuq_ref, wuk_ref, wuv_ref = refs[:10]
    refs = refs[10:]
    if rope:
        c, sa, sb = (r[...] for r in refs[:3])
        refs = refs[3:]
    q_ref, k_ref, v_ref = refs[:3]
    h = _rms_mod(y_ref[...], g_ref[...], sh_ref[...], sc_ref[...]).astype(BF16)
    down = _dot(h, wd_ref[...])
    cq = _rms(down[:, :Q_LORA], gq_ref[...])
    ckv = _rms(down[:, Q_LORA:Q_LORA + KV_LORA], gkv_ref[...])
    krb = down[:, MLA_KR_BLOCK:MLA_KR_BLOCK + LANE]
    if f32_out:
        ckv_ref, kr_ref = refs[3:5]
        ckv_ref[...] = ckv
        kr_ref[...] = krb
    if rope:
        krb = _rope_block(krb, c, sa, sb, ROPE_A // 2)
    cq16, ckv16 = cq.astype(BF16), ckv.astype(BF16)
    q = _dot(cq16, wuq_ref[...])
    kn = _dot(ckv16, wuk_ref[...])
    v_ref[...] = _dot(ckv16, wuv_ref[...]).astype(BF16)
    scale = (NOPE_A + ROPE_A) ** -0.5 * LOG2_E
    for hd in range(H_A):
        qh = q[:, hd * MLA_HEAD:(hd + 1) * MLA_HEAD]
        if rope:
            qh = _rope_block(qh, c, sa, sb, ROPE_A // 2)
        q_ref[:, hd * MLA_HEAD:(hd + 1) * MLA_HEAD] = (qh * scale).astype(BF16)
        k_ref[:, hd * MLA_HEAD:(hd + 1) * MLA_HEAD] = (kn[:, hd * MLA_HEAD:(hd + 1) * MLA_HEAD] + krb).astype(BF16)


def _mla_proj(y, norm_g, mod, layer, wd, gq, gkv, wuq, wuk, wuv, tables, latent):
    n_tok = N_LAT if latent else N_CTX
    t0 = N_CTX // TM if latent else 0
    nqk, nv = H_A * MLA_HEAD, H_A * V_A
    in_specs = [
        pl.BlockSpec((TM, D_MODEL), lambda t: (t + t0, 0)),
        pl.BlockSpec((None, 1, D_MODEL), lambda t: (layer, 0, 0)),
        _mod_spec(layer, 3, TM, t0),
        _mod_spec(layer, 4, TM, t0),
        _resident(wd.shape), _row_spec(Q_LORA), _row_spec(KV_LORA),
        _resident(wuq.shape), _resident(wuk.shape), _resident(wuv.shape),
    ]
    args = [y, norm_g.reshape(DEPTH, 1, D_MODEL), mod, mod, wd, gq, gkv, wuq, wuk, wuv]
    out_shape = [jax.ShapeDtypeStruct((n_tok, nqk), BF16),
                 jax.ShapeDtypeStruct((n_tok, nqk), BF16),
                 jax.ShapeDtypeStruct((n_tok, nv), BF16)]
    out_specs = [pl.BlockSpec((TM, nqk), lambda t: (t, 0)),
                 pl.BlockSpec((TM, nqk), lambda t: (t, 0)),
                 pl.BlockSpec((TM, nv), lambda t: (t, 0))]
    if latent:
        in_specs += _table_specs(TM)
        args += list(tables)
    else:
        out_shape += [jax.ShapeDtypeStruct((n_tok, KV_LORA), F32), jax.ShapeDtypeStruct((n_tok, LANE), F32)]
        out_specs += [pl.BlockSpec((TM, KV_LORA), lambda t: (t, 0)), pl.BlockSpec((TM, LANE), lambda t: (t, 0))]
    return pl.pallas_call(
        functools.partial(_mla_proj_kernel, rope=latent, f32_out=not latent),
        grid=(n_tok // TM,),
        in_specs=in_specs,
        out_specs=out_specs,
        out_shape=out_shape,
        compiler_params=_params("parallel"),
        name="mla_proj_lat" if latent else "mla_proj_ctx",
    )(*args)


def _mla_cache_kernel(ckv_ref, krb_ref, wuk_ref, wuv_ref, k_ref, v_ref):
    ckv16 = ckv_ref[...].astype(BF16)
    kn = _dot(ckv16, wuk_ref[...])
    v_ref[...] = _dot(ckv16, wuv_ref[...]).astype(BF16)
    krb = krb_ref[...]
    for hd in range(H_A):
        k_ref[:, hd * MLA_HEAD:(hd + 1) * MLA_HEAD] = (kn[:, hd * MLA_HEAD:(hd + 1) * MLA_HEAD] + krb).astype(BF16)


def _mla_cache_kv(ckv, krb, wuk, wuv):
    n = ckv.shape[0]
    nqk, nv = H_A * MLA_HEAD, H_A * V_A
    return pl.pallas_call(
        _mla_cache_kernel,
        grid=(n // TM,),
        in_specs=[pl.BlockSpec((TM, KV_LORA), lambda t: (t, 0)),
                  pl.BlockSpec((TM, LANE), lambda t: (t, 0)),
                  _resident(wuk.shape), _resident(wuv.shape)],
        out_specs=[pl.BlockSpec((TM, nqk), lambda t: (t, 0)), pl.BlockSpec((TM, nv), lambda t: (t, 0))],
        out_shape=[jax.ShapeDtypeStruct((n, nqk), BF16), jax.ShapeDtypeStruct((n, nv), BF16)],
        compiler_params=_params("parallel"),
        name="mla_cache_kv",
    )(ckv, krb, wuk, wuv)


def _scores_into(s_ref, q, k_refs):
    off = 0
    for k_ref in k_refs:
        n = k_ref.shape[0]
        s_ref[:, off:off + n] = _dot_nt(q, k_ref[...])
        off += n


def _softmax_parts(s):
    m = jnp.max(s, axis=-1, keepdims=True)
    e = jnp.exp2(s - m)
    return e, jnp.sum(e, axis=-1, keepdims=True)


def _weighted_values(w16, v_refs):
    off, o = 0, None
    for v_ref in v_refs:
        n = v_ref.shape[0]
        part = _dot(w16[:, off:off + n], v_ref[...])
        o = part if o is None else o + part
        off += n
    return o


def _head_out(s, v_refs):
    e, l = _softmax_parts(s)
    return _weighted_values(e.astype(BF16), v_refs) / l


def _lo_half(rows):
    return lax.broadcasted_iota(jnp.int32, (rows, LANE), 1) < LANE // 2


def _split_lanes(x, lo_half):
    z = jnp.zeros_like(x)
    return jnp.where(lo_half, x, z), jnp.where(lo_half, z, x)


def _lambda(lq1_ref, lk1_ref, lq2_ref, lk2_ref, lam_init):
    return (jnp.exp(jnp.sum(lq1_ref[...] * lk1_ref[...], axis=-1, keepdims=True))
            - jnp.exp(jnp.sum(lq2_ref[...] * lk2_ref[...], axis=-1, keepdims=True)) + lam_init)


def _diff_out(s0, s1, v_refs, lam, gsub, lam_init):
    e0, l0 = _softmax_parts(s0)
    e1, l1 = _softmax_parts(s1)
    a = e0 * (1.0 / l0) - e1 * (lam / l1)
    o = _weighted_values(a.astype(BF16), v_refs)
    return _rms(o, gsub) * (1.0 - lam_init)


def _tile_rows(i, size):
    start = i * size
    return pl.ds(start if isinstance(start, int) else pl.multiple_of(start, size), size)


def _fold_lanes(op, x, run):
    for j in range(x.shape[1] // LANE):
        blk = x[:, j * LANE:(j + 1) * LANE]
        run = blk if run is None else op(run, blk)
    return run


def _attn_phase(nxt, cur, v_refs):
    mrun = lrun = acc = None
    off = 0
    for seg, v_ref in enumerate(v_refs):
        n = v_ref.shape[0]
        chunk = min(KV_CHUNK, n)
        for c0 in range(0, n, chunk):
            cols = slice(off + c0, off + c0 + chunk)
            if nxt is not None:
                q, k_refs, s_ref, _ = nxt
                s_new = _dot_nt(q, k_refs[seg][c0:c0 + chunk, :])
                s_ref[:, cols] = s_new
                mrun = _fold_lanes(jnp.maximum, s_new, mrun)
            if cur is not None:
                blocks = [jnp.exp2(cur[0][:, lo:lo + LANE] - cur[1][...])
                          for lo in range(cols.start, cols.stop, LANE)]
                e = jnp.concatenate(blocks, axis=1)
                lrun = _fold_lanes(jnp.add, e, lrun)
                part = _dot(e.astype(BF16), v_ref[c0:c0 + chunk, :])
                acc = part if acc is None else acc + part
        off += n
    if nxt is not None:
        nxt[3][...] = jnp.broadcast_to(jnp.max(mrun, axis=-1, keepdims=True), mrun.shape)
    if cur is not None:
        return acc / jnp.sum(lrun, axis=-1, keepdims=True)


def _attn_lat_kernel(*refs, mask_q, lam_init):
    qa_ref, qb_ref, kal_ref, kac_ref, kbl_ref, kbc_ref, vl_ref, vc_ref = refs[:8]
    refs = refs[8:]
    if lam_init is not None:
        lq1_ref, lk1_ref, lq2_ref, lk2_ref, gsub_ref = refs[:5]
        refs = refs[5:]
        lam = _lambda(lq1_ref, lk1_ref, lq2_ref, lk2_ref, lam_init)
    o_ref, sa_ref, sb_ref, ma_ref, mb_ref = refs
    lo_half = _lo_half(TQ)
    n_tiles = qa_ref.shape[0] // TQ
    v_refs = (vl_ref, vc_ref)

    def rows(i):
        return _tile_rows(i, TQ)

    def next_a(i):
        qa = qa_ref[rows(i), :]
        return (_split_lanes(qa, lo_half)[0] if mask_q else qa, (kal_ref, kac_ref), sa_ref, ma_ref)

    def next_b(i):
        qb = qb_ref[rows(i), :]
        return (_split_lanes(qb, lo_half)[1] if mask_q else qb, (kbl_ref, kbc_ref), sb_ref, mb_ref)

    def emit(i, oa, ob):
        if lam_init is None:
            o = jnp.where(lo_half, oa, ob)
        else:
            o = _rms(oa - lam * ob, gsub_ref[...]) * (1.0 - lam_init)
        o_ref[rows(i), :] = o.astype(BF16)

    _attn_phase(next_a(0), None, v_refs)

    def body(i, carry):
        oa = _attn_phase(next_b(i), (sa_ref, ma_ref), v_refs)
        ob = _attn_phase(next_a(i + 1), (sb_ref, mb_ref), v_refs)
        emit(i, oa, ob)
        return carry

    lax.fori_loop(0, n_tiles - 1, body, 0)
    oa = _attn_phase(next_b(n_tiles - 1), (sa_ref, ma_ref), v_refs)
    ob = _attn_phase(None, (sb_ref, mb_ref), v_refs)
    emit(n_tiles - 1, oa, ob)


def _attn_lat(q, k_lat, k_cache, v_lat, v_cache, qa_blk, qb_blk, ka_blk, kb_blk, v_blk, mask_q, name, diff=None):
    nb, sq, _ = q.shape
    n_past = k_cache.shape[1]

    def spec(rows, blk):
        return pl.BlockSpec((None, rows, LANE), lambda b, p: (b, 0, blk(p)))

    in_specs = [spec(sq, qa_blk), spec(sq, qb_blk),
                spec(sq, ka_blk), spec(n_past, ka_blk), spec(sq, kb_blk), spec(n_past, kb_blk),
                spec(sq, v_blk), spec(n_past, v_blk)]
    args = [q, q, k_lat, k_cache, k_lat, k_cache, v_lat, v_cache]
    lam_init = None
    if diff is not None:
        lvec, lam_init = diff
        in_specs += [_row_spec(DK_C)] * 4 + [_row_spec(DV_C)]
        args += list(lvec)
    return pl.pallas_call(
        functools.partial(_attn_lat_kernel, mask_q=mask_q, lam_init=lam_init),
        grid=(nb, D_MODEL // LANE),
        in_specs=in_specs,
        out_specs=pl.BlockSpec((None, sq, LANE), lambda b, p: (b, 0, p)),
        out_shape=jax.ShapeDtypeStruct((nb, sq, D_MODEL), BF16),
        scratch_shapes=[pltpu.VMEM((TQ, sq + n_past), F32)] * 2 + [pltpu.VMEM((TQ, LANE), F32)] * 2,
        compiler_params=_params("parallel", "parallel"),
        name=name,
    )(*args)


def _attn_ctx_kernel(q_ref, k_ref, v_ref, o_ref, *, qa_blk, qb_blk, ka_blk, kb_blk, v_blk, mask_q):
    lo_half = _lo_half(q_ref.shape[0])

    def blk(ref, b):
        return ref[:, b * LANE:(b + 1) * LANE]

    for p in range(D_MODEL // LANE):
        qa, qb = blk(q_ref, qa_blk(p)), blk(q_ref, qb_blk(p))
        if mask_q:
            qa, qb = _split_lanes(qa, lo_half)[0], _split_lanes(qb, lo_half)[1]
        v = v_ref.at[:, v_blk(p) * LANE:(v_blk(p) + 1) * LANE]
        oa = _head_out(_dot_nt(qa, blk(k_ref, ka_blk(p))), (v,))
        ob = _head_out(_dot_nt(qb, blk(k_ref, kb_blk(p))), (v,))
        o_ref[:, p * LANE:(p + 1) * LANE] = jnp.where(lo_half, oa, ob).astype(BF16)


def _attn_ctx(q, k, v, qa_blk, qb_blk, ka_blk, kb_blk, v_blk, mask_q, name):
    nb, sq, _ = q.shape

    def spec(x):
        return pl.BlockSpec((None,) + x.shape[1:], lambda b: (b, 0, 0))

    return pl.pallas_call(
        functools.partial(_attn_ctx_kernel, qa_blk=qa_blk, qb_blk=qb_blk, ka_blk=ka_blk, kb_blk=kb_blk,
                          v_blk=v_blk, mask_q=mask_q),
        grid=(nb,),
        in_specs=[spec(q), spec(k), spec(v)],
        out_specs=pl.BlockSpec((None, sq, D_MODEL), lambda b: (b, 0, 0)),
        out_shape=jax.ShapeDtypeStruct((nb, sq, D_MODEL), BF16),
        compiler_params=_params("parallel"),
        name=name,
    )(q, k, v)


def _attn_diff_lat(q, k_lat, k_cache, v_lat, v_cache, lvec, lam_init):
    same = lambda h: h
    return _attn_lat(q, k_lat, k_cache, v_lat, v_cache, same, same, same, same, same, True, "attn_diff_lat",
                     diff=(lvec, lam_init))


def _attn_diff_ctx_kernel(q_ref, k_ref, v_ref, lq1_ref, lk1_ref, lq2_ref, lk2_ref, gsub_ref, o_ref, *, lam_init):
    lo_half = _lo_half(q_ref.shape[0])
    lam = _lambda(lq1_ref, lk1_ref, lq2_ref, lk2_ref, lam_init)
    for h in range(H_C):
        cols = slice(h * LANE, (h + 1) * LANE)
        q0, q1 = _split_lanes(q_ref[:, cols], lo_half)
        k = k_ref[:, cols]
        o = _diff_out(_dot_nt(q0, k), _dot_nt(q1, k), (v_ref.at[:, cols],), lam, gsub_ref[...], lam_init)
        o_ref[:, cols] = o.astype(BF16)


def _attn_diff_ctx(q, k, v, lvec, lam_init):
    nb, sq, _ = q.shape
    spec = pl.BlockSpec((None, sq, D_MODEL), lambda b: (b, 0, 0))
    vec = _row_spec(DK_C)
    return pl.pallas_call(
        functools.partial(_attn_diff_ctx_kernel, lam_init=lam_init),
        grid=(nb,),
        in_specs=[spec, spec, spec, vec, vec, vec, vec, _row_spec(DV_C)],
        out_specs=spec,
        out_shape=jax.ShapeDtypeStruct((nb, sq, D_MODEL), BF16),
        compiler_params=_params("parallel"),
        name="attn_diff_ctx",
    )(q, k, v, *lvec)


def _final_kernel(y_ref, g_ref, o_ref):
    o_ref[...] = _rms(y_ref[...], g_ref[...])


def _final_norm(y, g, n_tok, t0):
    return pl.pallas_call(
        _final_kernel,
        grid=(n_tok // TM,),
        in_specs=[pl.BlockSpec((TM, D_MODEL), lambda t: (t + t0, 0)), _row_spec(D_MODEL)],
        out_specs=pl.BlockSpec((TM, D_MODEL), lambda t: (t, 0)),
        out_shape=jax.ShapeDtypeStruct((n_tok, D_MODEL), F32),
        compiler_params=_params("parallel"),
        name="final_norm",
    )(y, g)


def _pad_heads(w, n_heads, width):
    k = w.shape[0]
    w = w.reshape(k, n_heads, width)
    return jnp.pad(w, ((0, 0), (0, 0), (0, MLA_HEAD - width))).reshape(k, n_heads * MLA_HEAD)


def _mla_down_layout(w):
    z = lambda n: jnp.zeros((D_MODEL, n), w.dtype)
    return jnp.concatenate([w[:, :Q_LORA + KV_LORA], z(NOPE_A), w[:, Q_LORA + KV_LORA:],
                            z(LANE - NOPE_A - ROPE_A)], axis=1)


def kernel(x_prompt, x_sample, cache_mla_ckv, cache_mla_kr, cache_gqa_k, cache_gqa_v, cache_diff_k, cache_diff_v, c, c_ctx, ada_w, ada_b, norm_ffn1, norm_mix, norm_ffn2, ffn1_w_in, ffn1_w_out, ffn2_w_in, ffn2_w_out, mla_w_down, mla_g_q, mla_g_kv, mla_w_uq, mla_w_uk, mla_w_uv, mla_w_o, gqa_w_qkv, gqa_g_q, gqa_g_k, gqa_w_o, diff_w_qkv, diff_lq1, diff_lk1, diff_lq2, diff_lk2, diff_g_sub, diff_w_o, norm_final):
    y_parts = (x_prompt.reshape(N_CTX, D_MODEL), x_sample.reshape(N_LAT, D_MODEL))
    c_rows = jnp.concatenate([c_ctx[None, :], c, jnp.zeros((MOD_ROWS - 1 - DEC_BATCH, D_MODEL), F32)], axis=0)
    mod = _ada(c_rows, ada_w, ada_b)

    w1_in, w1_out = ffn1_w_in.astype(BF16), ffn1_w_out.astype(BF16)
    w2_in, w2_out = ffn2_w_in.astype(BF16), ffn2_w_out.astype(BF16)
    tab_a = _rope_tables(ROPE_A, NOPE_A, LANE - NOPE_A - ROPE_A, 1)
    tab_64 = _rope_tables(HD_B, 0, 0, 2)
    idx = jnp.arange(H_B * HD_B) // HD_B
    grp = jnp.where(idx[:, None] == idx[None, :], 1.0 / HD_B, 0.0).astype(BF16)

    new_ckv, new_kr, new_gk, new_gv, new_dk, new_dv = [], [], [], [], [], []
    for i in range(DEPTH):
        kind, j = i % N_MIXERS, i // N_MIXERS
        y = _ffn(y_parts, norm_ffn1, mod, i, 0, w1_in, w1_out)
        if kind == 0:
            wd = _mla_down_layout(mla_w_down[j]).astype(BF16)
            wuq = _pad_heads(mla_w_uq[j], H_A, NOPE_A + ROPE_A).astype(BF16)
            wuk = _pad_heads(mla_w_uk[j], H_A, NOPE_A).astype(BF16)
            wuv = mla_w_uv[j].astype(BF16)
            gq, gkv = mla_g_q[j][None, :], mla_g_kv[j][None, :]
            qc, kc, vc, ckv, krb = _mla_proj(y, norm_mix, mod, i, wd, gq, gkv, wuq, wuk, wuv, None, False)
            new_ckv.append(ckv.reshape(BATCH, SEQ, KV_LORA))
            new_kr.append(krb[:, NOPE_A:NOPE_A + ROPE_A].reshape(BATCH, SEQ, ROPE_A))
            ql, kl, vl = _mla_proj(y, norm_mix, mod, i, wd, gq, gkv, wuq, wuk, wuv, tab_a, True)
            kr_pad = jnp.pad(cache_mla_kr[:, j].reshape(DEC_BATCH * PAST_LEN, ROPE_A),
                             ((0, 0), (NOPE_A, LANE - NOPE_A - ROPE_A)))
            kp, vp = _mla_cache_kv(cache_mla_ckv[:, j].reshape(DEC_BATCH * PAST_LEN, KV_LORA), kr_pad, wuk, wuv)
            blocks = (lambda p: 2 * p, lambda p: 2 * p + 1, lambda p: 2 * p, lambda p: 2 * p + 1, lambda p: p)
            mask_q = False
            w_o = mla_w_o[j]
        elif kind == 1:
            w = gqa_w_qkv[j].astype(BF16)
            gq = jnp.tile(gqa_g_q[j], H_B)[None, :]
            gk = jnp.tile(gqa_g_k[j], KV_B)[None, :]
            qc, kc, vc, kf, vf = _gqa_proj(y, norm_mix, mod, i, w, grp, gq, gk, None, False)
            new_gk.append(kf.reshape(BATCH, SEQ, KV_B, HD_B))
            new_gv.append(vf.reshape(BATCH, SEQ, KV_B, HD_B))
            ql, kl, vl = _gqa_proj(y, norm_mix, mod, i, w, grp, gq, gk, tab_64, True)

            def dup(x):
                x = x.reshape(DEC_BATCH * PAST_LEN, KV_B, 1, HD_B)
                return jnp.broadcast_to(x, (DEC_BATCH * PAST_LEN, KV_B, 2, HD_B)).reshape(
                    DEC_BATCH * PAST_LEN, 2 * KV_B * HD_B).astype(BF16)

            kp, vp = dup(cache_gqa_k[:, j]), dup(cache_gqa_v[:, j])
            blocks = (lambda p: p, lambda p: p, lambda p: p // 2, lambda p: p // 2, lambda p: p // 2)
            mask_q = True
            w_o = gqa_w_o[j]
        else:
            w = diff_w_qkv[j].astype(BF16)
            qc, kc, vc, kf, vf = _diff_proj(y, norm_mix, mod, i, w, None, False)
            new_dk.append(kf.reshape(BATCH, SEQ, H_C, 2, DK_C))
            new_dv.append(vf.reshape(BATCH, SEQ, H_C, DV_C))
            ql, kl, vl = _diff_proj(y, norm_mix, mod, i, w, tab_64, True)
            kp = cache_diff_k[:, j].reshape(DEC_BATCH * PAST_LEN, H_C * 2 * DK_C).astype(BF16)
            vp = cache_diff_v[:, j].reshape(DEC_BATCH * PAST_LEN, H_C * DV_C).astype(BF16)
            w_o = diff_w_o[j]

        def b3(x, nb):
            return x.reshape(nb, x.shape[0] // nb, x.shape[1])

        lat_kv = (b3(kl, DEC_BATCH), b3(kp, DEC_BATCH), b3(vl, DEC_BATCH), b3(vp, DEC_BATCH))
        if kind == 2:
            lam_init = 0.8 - 0.6 * math.exp(-0.3 * i)
            lvec = (diff_lq1[j][None, :], diff_lk1[j][None, :], diff_lq2[j][None, :], diff_lk2[j][None, :],
                    diff_g_sub[j][None, :])
            o_ctx = _attn_diff_ctx(b3(qc, BATCH), b3(kc, BATCH), b3(vc, BATCH), lvec, lam_init)
            o_lat = _attn_diff_lat(b3(ql, DEC_BATCH), *lat_kv, lvec, lam_init)
        else:
            o_ctx = _attn_ctx(b3(qc, BATCH), b3(kc, BATCH), b3(vc, BATCH), *blocks, mask_q, "attn_ctx")
            o_lat = _attn_lat(b3(ql, DEC_BATCH), *lat_kv, *blocks, mask_q, "attn_lat")
        attn = (o_ctx.reshape(N_CTX, D_MODEL), o_lat.reshape(N_LAT, D_MODEL), w_o.astype(BF16))
        y = _ffn((y,), norm_ffn2, mod, i, 6, w2_in, w2_out, attn=attn)
        y_parts = (y,)

    g_final = norm_final[None, :]
    y_prompt = _final_norm(y, g_final, N_CTX, 0).reshape(BATCH, SEQ, D_MODEL)
    y_sample = _final_norm(y, g_final, N_LAT, N_CTX // TM).reshape(DEC_BATCH, DEC_SEQ, D_MODEL)
    return (y_prompt, y_sample,
            jnp.stack(new_ckv, axis=1), jnp.stack(new_kr, axis=1),
            jnp.stack(new_gk, axis=1), jnp.stack(new_gv, axis=1),
            jnp.stack(new_dk, axis=1), jnp.stack(new_dv, axis=1))
```
